```python
import math
import jax, jax.numpy as jnp
from jax import lax
import numpy as np

D_MODEL = 1024
BATCH = 2
SEQ = 16384
DEPTH = 2

D_FF = 2816
A_CHUNK = 128
A_GROUPS = 4
A_GROUP_DIM = 128
A_WIDTH = A_GROUPS * A_GROUP_DIM
B_WINDOW = 128
B_Q_HEADS = 8
B_KV_HEADS = 2
B_HEAD_DIM = 64
B_Q_PER_KV = B_Q_HEADS // B_KV_HEADS
B_WIDTH = B_Q_HEADS * B_HEAD_DIM
B_KV_WIDTH = B_KV_HEADS * B_HEAD_DIM
C_CHUNK = 128
C_HEADS = 4
C_QK_DIM = 128
C_V_DIM = 256
C_QK_WIDTH = C_HEADS * C_QK_DIM
C_V_WIDTH = C_HEADS * C_V_DIM
SPLIT_SIZES = (A_WIDTH, A_WIDTH,
               B_WIDTH, B_KV_WIDTH, B_KV_WIDTH,
               C_QK_WIDTH, C_QK_WIDTH, C_V_WIDTH, C_V_WIDTH,
               3 * D_MODEL)
IN_WIDTH = sum(SPLIT_SIZES)
NORM_EPS = 1e-6
GN_EPS = 1e-5

kernel_name = "hybrid_gmlp_swa_retention_macaron"


def rmsnorm(x, g):
    xf = x.astype(jnp.float32)
    y = xf * lax.rsqrt(jnp.mean(xf * xf, axis=-1, keepdims=True) + NORM_EPS)
    return (y * g.astype(jnp.float32)).astype(x.dtype)


def swiglu(h, w_in, w_out):
    a, b = jnp.split(h @ w_in, 2, axis=-1)
    return (jax.nn.silu(a) * b) @ w_out


def gmlp_spatial(z, v_norm, w_s, b_s):
    bsz, t_len, _ = z.shape
    u, v = jnp.split(z, 2, axis=-1)
    v = rmsnorm(v, v_norm)
    vc = v.reshape(bsz, t_len // A_CHUNK, A_CHUNK, A_GROUPS, A_GROUP_DIM)
    causal = jnp.tril(jnp.ones((A_CHUNK, A_CHUNK), dtype=bool))
    w = jnp.where(causal[None], w_s, jnp.zeros_like(w_s))
    s = jnp.einsum('gts,bcsgd->bctgd', w, vc) + b_s.T[None, None, :, :, None]
    return u * s.reshape(bsz, t_len, A_WIDTH)


def swa_attention(q, k, v, sinks):
    bsz, t_len, _ = q.shape
    n = t_len // B_WINDOW
    W = B_WINDOW
    qb = q.reshape(bsz, n, W, B_KV_HEADS, B_Q_PER_KV, B_HEAD_DIM)
    kb = k.reshape(bsz, n, W, B_KV_HEADS, B_HEAD_DIM)
    vb = v.reshape(bsz, n, W, B_KV_HEADS, B_HEAD_DIM)
    pad = ((0, 0), (1, 0), (0, 0), (0, 0), (0, 0))
    kk = jnp.concatenate([jnp.pad(kb, pad)[:, :-1], kb], axis=2)
    vv = jnp.concatenate([jnp.pad(vb, pad)[:, :-1], vb], axis=2)
    scores = jnp.einsum('bnqhgd,bnkhd->bnhgqk', qb, kk).astype(jnp.float32)
    scores = scores * (1.0 / math.sqrt(B_HEAD_DIM))
    qpos = jnp.arange(W)[:, None] + W
    kpos = jnp.arange(2 * W)[None, :]
    dist = (qpos - kpos).astype(jnp.float32)
    valid = (dist >= 0) & (dist < W)
    block_valid = valid[None] & ((jnp.arange(n) > 0)[:, None, None] | (kpos >= W)[None])
    slopes = jnp.exp2(-8.0 / B_Q_HEADS * (jnp.arange(B_Q_HEADS, dtype=jnp.float32) + 1.0))
    alibi = -slopes.reshape(B_KV_HEADS, B_Q_PER_KV)[:, :, None, None] * dist
    scores = jnp.where(block_valid[None, :, None, None], scores + alibi[None, None], -jnp.inf)
    sink = sinks.astype(jnp.float32).reshape(B_KV_HEADS, B_Q_PER_KV)[None, None, :, :, None, None]
    m = jnp.maximum(jnp.max(scores, axis=-1, keepdims=True), sink)
    p = jnp.exp(scores - m)
    probs = p / (jnp.sum(p, axis=-1, keepdims=True) + jnp.exp(sink - m))
    out = jnp.einsum('bnhgqk,bnkhd->bnqhgd', probs.astype(v.dtype), vv)
    return out.reshape(bsz, t_len, B_WIDTH)


def retention(q, k, v, g, gn_gain):
    bsz, t_len, _ = q.shape
    n = t_len // C_CHUNK
    dt = q.dtype
    log_gamma = jnp.log(1.0 - jnp.exp2(-5.0 - jnp.arange(C_HEADS, dtype=jnp.float32)))
    qc = q.reshape(bsz, n, C_CHUNK, C_HEADS, C_QK_DIM)
    kc = k.reshape(bsz, n, C_CHUNK, C_HEADS, C_QK_DIM) * (C_QK_DIM ** -0.5)
    vc = v.reshape(bsz, n, C_CHUNK, C_HEADS, C_V_DIM)
    idx = jnp.arange(C_CHUNK, dtype=jnp.float32)
    dist = idx[:, None] - idx[None, :]
    decay = jnp.where(dist[None] >= 0, jnp.exp(log_gamma[:, None, None] * jnp.maximum(dist, 0.0)[None]), 0.0)
    scores = jnp.einsum('bnthd,bnshd->bnhts', qc, kc) * decay.astype(dt)[None, None]
    inner = jnp.einsum('bnhts,bnshe->bnthe', scores, vc)
    k_w = jnp.exp(log_gamma[None, :] * (C_CHUNK - 1.0 - idx)[:, None]).astype(dt)
    kv = jnp.einsum('bnshd,bnshe->nbhde', kc * k_w[None, None, :, :, None], vc)
    chunk_decay = jnp.exp(log_gamma * C_CHUNK).astype(dt)[None, :, None, None]

    def step(state, kv_i):
        return chunk_decay * state + kv_i, state

    init = jnp.zeros((bsz, C_HEADS, C_QK_DIM, C_V_DIM), dtype=kv.dtype)
    _, s_prev = lax.scan(step, init, kv)
    q_w = jnp.exp(log_gamma[None, :] * (idx + 1.0)[:, None]).astype(dt)
    cross = jnp.einsum('bnthd,nbhde->bnthe', qc * q_w[None, None, :, :, None], s_prev)
    o = (inner + cross).astype(jnp.float32)
    mu = jnp.mean(o, axis=-1, keepdims=True)
    var = jnp.mean(jnp.square(o - mu), axis=-1, keepdims=True)
    o = ((o - mu) * lax.rsqrt(var + GN_EPS)).reshape(bsz, t_len, C_V_WIDTH)
    o = (o * gn_gain.astype(jnp.float32)).astype(dt)
    return o * jax.nn.silu(g)


def hybrid_layer(x, ffn1_norm, ffn1_w_in, ffn1_w_out, mix_norm, w_in, b_gate,
                 gmlp_v_norm, gmlp_w_s, gmlp_b_s, attn_sinks, ret_gn,
                 w_branch_a, w_branch_b, w_branch_c, w_out,
                 ffn2_norm, ffn2_w_in, ffn2_w_out):
    x = x + 0.5 * swiglu(rmsnorm(x, ffn1_norm), ffn1_w_in, ffn1_w_out)
    h = rmsnorm(x, mix_norm)
    p = h @ w_in
    cuts = list(np.cumsum(SPLIT_SIZES)[:-1])
    a_u, a_v, b_q, b_k, b_v, c_q, c_k, c_v, c_g, gate_pre = jnp.split(p, cuts, axis=-1)
    y_a = gmlp_spatial(jax.nn.gelu(jnp.concatenate([a_u, a_v], axis=-1)),
                       gmlp_v_norm, gmlp_w_s, gmlp_b_s) @ w_branch_a
    y_b = swa_attention(b_q, b_k, b_v, attn_sinks) @ w_branch_b
    y_c = retention(c_q, c_k, c_v, c_g, ret_gn) @ w_branch_c
    g_a, g_b, g_c = jnp.split(jax.nn.sigmoid(gate_pre + b_gate), 3, axis=-1)
    merged = g_a * y_a + g_b * y_b + g_c * y_c
    x = x + merged @ w_out
    x = x + 0.5 * swiglu(rmsnorm(x, ffn2_norm), ffn2_w_in, ffn2_w_out)
    return x


def setup_inputs(seed: int = 0) -> dict:
    key = jax.random.key(seed)
    ks = jax.random.split(key, 24)
    L, D, F = DEPTH, D_MODEL, D_FF
    f32 = jnp.float32

    def nrm(k, shape, fan_in):
        return jax.random.normal(k, shape, f32) * (fan_in ** -0.5)

    def gain(k, shape):
        return 1.0 + 0.02 * jax.random.normal(k, shape, f32)

    return {
        "x": jax.random.normal(ks[0], (BATCH, SEQ, D), f32),
        "ffn1_norm": gain(ks[1], (L, D)),
        "ffn1_w_in": nrm(ks[2], (L, D, 2 * F), D),
        "ffn1_w_out": nrm(ks[3], (L, F, D), F),
        "mix_norm": gain(ks[4], (L, D)),
        "w_in": nrm(ks[5], (L, D, IN_WIDTH), D),
        "b_gate": 0.02 * jax.random.normal(ks[6], (L, 3 * D), f32),
        "gmlp_v_norm": gain(ks[7], (L, A_WIDTH)),
        "gmlp_w_s": nrm(ks[8], (L, A_GROUPS, A_CHUNK, A_CHUNK), A_CHUNK),
        "gmlp_b_s": 1.0 + 0.02 * jax.random.normal(ks[9], (L, A_GROUPS, A_CHUNK), f32),
        "attn_sinks": 0.5 * jax.random.normal(ks[10], (L, B_Q_HEADS), f32),
        "ret_gn": gain(ks[11], (L, C_V_WIDTH)),
        "w_branch_a": nrm(ks[12], (L, A_WIDTH, D), A_WIDTH),
        "w_branch_b": nrm(ks[13], (L, B_WIDTH, D), B_WIDTH),
        "w_branch_c": nrm(ks[14], (L, C_V_WIDTH, D), C_V_WIDTH),
        "w_out": nrm(ks[15], (L, D, D), D),
        "ffn2_norm": gain(ks[16], (L, D)),
        "ffn2_w_in": nrm(ks[17], (L, D, 2 * F), D),
        "ffn2_w_out": nrm(ks[18], (L, F, D), F),
        "final_norm": gain(ks[19], (D,)),
    }


def reference(x, ffn1_norm, ffn1_w_in, ffn1_w_out, mix_norm, w_in, b_gate,
              gmlp_v_norm, gmlp_w_s, gmlp_b_s, attn_sinks, ret_gn,
              w_branch_a, w_branch_b, w_branch_c, w_out,
              ffn2_norm, ffn2_w_in, ffn2_w_out, final_norm):
    for l in range(DEPTH):
        x = hybrid_layer(x, ffn1_norm[l], ffn1_w_in[l], ffn1_w_out[l], mix_norm[l],
                         w_in[l], b_gate[l], gmlp_v_norm[l], gmlp_w_s[l], gmlp_b_s[l],
                         attn_sinks[l], ret_gn[l], w_branch_a[l], w_branch_b[l],
                         w_branch_c[l], w_out[l], ffn2_norm[l], ffn2_w_in[l], ffn2_w_out[l])
    return rmsnorm(x, final_norm)
```

```python
import functools
import math

import jax
import jax.numpy as jnp
import numpy as np
from jax import lax
from jax.experimental import pallas as pl
from jax.experimental.pallas import tpu as pltpu

D_MODEL = 1024
D_FF = 2816
CHUNK = 128
A_GROUPS = 4
A_WIDTH = 512
B_Q_HEADS = 8
B_KV_HEADS = 2
B_Q_PER_KV = B_Q_HEADS // B_KV_HEADS
B_HEAD_DIM = 64
B_WIDTH = B_Q_HEADS * B_HEAD_DIM
B_KV_WIDTH = B_KV_HEADS * B_HEAD_DIM
C_HEADS = 4
C_QK_DIM = 128
C_V_DIM = 256
C_QK_WIDTH = C_HEADS * C_QK_DIM
C_V_WIDTH = C_HEADS * C_V_DIM
NORM_EPS = 1e-6
GN_EPS = 1e-5

OFF_AU = 0
OFF_AV = OFF_AU + A_WIDTH
OFF_BQ = OFF_AV + A_WIDTH
OFF_BK = OFF_BQ + B_WIDTH
OFF_BV = OFF_BK + B_KV_WIDTH
OFF_CQ = OFF_BV + B_KV_WIDTH
OFF_CK = OFF_CQ + C_QK_WIDTH
OFF_CV = OFF_CK + C_QK_WIDTH
OFF_CG = OFF_CV + C_V_WIDTH
OFF_GATE = OFF_CG + C_V_WIDTH
IN_WIDTH = OFF_GATE + 3 * D_MODEL

FFN_TOKENS = 512
FFN_COLS = 256
MIX_TOKENS = 256
V7X_VMEM_LIMIT = 56 * 1024 * 1024

BF16 = jnp.bfloat16
F32 = jnp.float32


def _dot(a, b):
    return jnp.dot(a, b, preferred_element_type=F32)


def _dot_nt(a, b):
    return lax.dot_general(a, b, (((1,), (1,)), ((), ())), preferred_element_type=F32)


def _dot_tn(a, b):
    return lax.dot_general(a, b, (((0,), (0,)), ((), ())), preferred_element_type=F32)


def _rmsnorm(x, gain):
    return x * lax.rsqrt(jnp.mean(x * x, axis=-1, keepdims=True) + NORM_EPS) * gain


def _resident(shape):
    zeros = (0,) * len(shape)
    return pl.BlockSpec(shape, lambda *_: zeros, pipeline_mode=pl.Buffered(1))


def _ffn_kernel(x_ref, gain_ref, w_in_ref, w_out_ref, fgain_ref, o_ref, *, final_norm):
    x = x_ref[...]
    h = _rmsnorm(x, gain_ref[...]).astype(BF16)
    acc = None
    for c in range(D_FF // FFN_COLS):
        lo = c * FFN_COLS
        a = _dot(h, w_in_ref[:, lo:lo + FFN_COLS])
        b = _dot(h, w_in_ref[:, D_FF + lo:D_FF + lo + FFN_COLS])
        act = (a * jax.nn.sigmoid(a) * b).astype(BF16)
        part = _dot(act, w_out_ref[lo:lo + FFN_COLS, :])
        acc = part if acc is None else acc + part
    y = x + 0.5 * acc
    if final_norm:
        y = _rmsnorm(y, fgain_ref[...])
    o_ref[...] = y


def _ffn(x2d, gain, w_in, w_out, fgain, final_norm):
    n_tok = x2d.shape[0]
    return pl.pallas_call(
        functools.partial(_ffn_kernel, final_norm=final_norm),
        grid=(n_tok // FFN_TOKENS,),
        in_specs=[
            pl.BlockSpec((FFN_TOKENS, D_MODEL), lambda i: (i, 0)),
            _resident((1, D_MODEL)),
            _resident((D_MODEL, 2 * D_FF)),
            _resident((D_FF, D_MODEL)),
            _resident((1, D_MODEL)),
        ],
        out_specs=pl.BlockSpec((FFN_TOKENS, D_MODEL), lambda i: (i, 0)),
        out_shape=jax.ShapeDtypeStruct(x2d.shape, F32),
        compiler_params=pltpu.CompilerParams(
            dimension_semantics=("arbitrary",), vmem_limit_bytes=V7X_VMEM_LIMIT),
        name="ffn_final" if final_norm else "ffn",
    )(x2d, gain, w_in, w_out, fgain)


def _mixer_kernel(sink_ref, x_ref, gain_ref, w_in_ref, bgate_ref, vnorm_ref, ws_ref, bs_ref,
                  gn_ref, wa_ref, wb_ref, wc_ref, wo_ref, decay_ref, kw_ref, qw_ref, bias_ref,
                  o_ref,
                  u_s, v_s, q_s, k_s, vv_s, cq_s, ck_s, cv_s, sg_s, ya_s, yb_s, yc_s, state_s,
                  *, chunk_decay):
    t = pl.program_id(1)
    n_chunks = MIX_TOKENS // CHUNK

    @pl.when(t == 0)
    def _():
        state_s[...] = jnp.zeros_like(state_s)
        k_s[0:CHUNK, :] = jnp.zeros((CHUNK, B_KV_WIDTH), BF16)
        vv_s[0:CHUNK, :] = jnp.zeros((CHUNK, B_KV_WIDTH), BF16)

    x = x_ref[...]
    h = _rmsnorm(x, gain_ref[...]).astype(BF16)

    def proj(lo, width):
        return _dot(h, w_in_ref[:, lo:lo + width])

    z = jax.nn.gelu(proj(OFF_AU, 2 * A_WIDTH))
    u_s[...] = z[:, :A_WIDTH]
    v_s[...] = _rmsnorm(z[:, A_WIDTH:], vnorm_ref[...]).astype(BF16)
    q_s[...] = (proj(OFF_BQ, B_WIDTH) * (1.0 / math.sqrt(B_HEAD_DIM))).astype(BF16)
    k_s[CHUNK:, :] = proj(OFF_BK, B_KV_WIDTH).astype(BF16)
    vv_s[CHUNK:, :] = proj(OFF_BV, B_KV_WIDTH).astype(BF16)
    cq_s[...] = proj(OFF_CQ, C_QK_WIDTH)
    ck_s[...] = proj(OFF_CK, C_QK_WIDTH) * (C_QK_DIM ** -0.5)
    cv_s[...] = proj(OFF_CV, C_V_WIDTH).astype(BF16)
    cg = proj(OFF_CG, C_V_WIDTH)
    sg_s[...] = cg * jax.nn.sigmoid(cg)

    row = lax.broadcasted_iota(jnp.int32, (CHUNK, CHUNK), 0)
    col = lax.broadcasted_iota(jnp.int32, (CHUNK, CHUNK), 1)
    causal = row >= col
    w_spatial = [jnp.where(causal, ws_ref[g], 0.0).astype(BF16) for g in range(A_GROUPS)]
    prev_cols = lax.broadcasted_iota(jnp.int32, (1, 2 * CHUNK), 1) < CHUNK

    def chunk_body(c, carry):
        r0 = pl.multiple_of(c * CHUNK, CHUNK)
        rows = pl.ds(r0, CHUNK)

        for g in range(A_GROUPS):
            cols = slice(g * CHUNK, (g + 1) * CHUNK)
            s = _dot(w_spatial[g], v_s[rows, cols]) + bs_ref[g]
            ya_s[rows, cols] = (u_s[rows, cols] * s).astype(BF16)

        first = jnp.logical_and(t == 0, c == 0)
        hide_prev = jnp.logical_and(first, prev_cols)
        for hk in range(B_KV_HEADS):
            kcols = slice(hk * B_HEAD_DIM, (hk + 1) * B_HEAD_DIM)
            k2 = k_s[pl.ds(r0, 2 * CHUNK), kcols]
            v2 = vv_s[pl.ds(r0, 2 * CHUNK), kcols]
            heads = [hk * B_Q_PER_KV + g for g in range(B_Q_PER_KV)]
            qst = jnp.concatenate(
                [q_s[rows, j * B_HEAD_DIM:(j + 1) * B_HEAD_DIM] for j in heads], axis=0)
            scores = _dot_nt(qst, k2) + bias_ref[hk]
            scores = jnp.where(hide_prev, -jnp.inf, scores)
            probs, denoms = [], []
            for g, j in enumerate(heads):
                sc = scores[g * CHUNK:(g + 1) * CHUNK]
                sink = sink_ref[j]
                m = jnp.maximum(jnp.max(sc, axis=-1, keepdims=True), sink)
                p = jnp.exp(sc - m)
                denoms.append(jnp.sum(p, axis=-1, keepdims=True) + jnp.exp(sink - m))
                probs.append(p.astype(BF16))
            out = _dot(jnp.concatenate(probs, axis=0), v2)
            for g, j in enumerate(heads):
                yb_s[rows, j * B_HEAD_DIM:(j + 1) * B_HEAD_DIM] = (
                    out[g * CHUNK:(g + 1) * CHUNK] / denoms[g]).astype(BF16)

        for hd in range(C_HEADS):
            qk = slice(hd * C_QK_DIM, (hd + 1) * C_QK_DIM)
            vc = slice(hd * C_V_DIM, (hd + 1) * C_V_DIM)
            qh = cq_s[rows, qk]
            kh = ck_s[rows, qk]
            vh = cv_s[rows, vc]
            sc = _dot_nt(qh.astype(BF16), kh.astype(BF16)) * decay_ref[hd]
            inner = _dot(sc.astype(BF16), vh)
            state = state_s[hd]
            cross = _dot((qh * qw_ref[hd]).astype(BF16), state.astype(BF16))
            o = inner + cross
            mu = jnp.mean(o, axis=-1, keepdims=True)
            var = jnp.mean(jnp.square(o - mu), axis=-1, keepdims=True)
            o = (o - mu) * lax.rsqrt(var + GN_EPS) * gn_ref[:, vc]
            yc_s[rows, vc] = (o * sg_s[rows, vc]).astype(BF16)
            kv = _dot_tn((kh * kw_ref[hd]).astype(BF16), vh)
            state_s[hd] = chunk_decay[hd] * state + kv
        return carry

    lax.fori_loop(0, n_chunks, chunk_body, 0)

    k_s[0:CHUNK, :] = k_s[MIX_TOKENS:, :]
    vv_s[0:CHUNK, :] = vv_s[MIX_TOKENS:, :]

    merged = None
    for j, (y_s, w_ref) in enumerate(((ya_s, wa_ref), (yb_s, wb_ref), (yc_s, wc_ref))):
        lo = j * D_MODEL
        gate = jax.nn.sigmoid(proj(OFF_GATE + lo, D_MODEL) + bgate_ref[:, lo:lo + D_MODEL])
        term = gate * _dot(y_s[...], w_ref[...])
        merged = term if merged is None else merged + term
    o_ref[...] = x + _dot(merged.astype(BF16), wo_ref[...])


def _retention_constants():
    gamma = 1.0 - np.exp2(-5.0 - np.arange(C_HEADS, dtype=np.float64))
    log_gamma = np.log(gamma)
    idx = np.arange(CHUNK, dtype=np.float64)
    dist = idx[:, None] - idx[None, :]
    decay = np.where(dist[None] >= 0, np.exp(log_gamma[:, None, None] * np.maximum(dist, 0.0)[None]), 0.0)
    k_w = np.exp(log_gamma[:, None] * (CHUNK - 1.0 - idx)[None, :])
    q_w = np.exp(log_gamma[:, None] * (idx + 1.0)[None, :])
    k_w = np.broadcast_to(k_w[:, :, None], (C_HEADS, CHUNK, C_QK_DIM))
    q_w = np.broadcast_to(q_w[:, :, None], (C_HEADS, CHUNK, C_QK_DIM))
    chunk_decay = tuple(float(v) for v in np.exp(log_gamma * CHUNK))
    as32 = lambda a: jnp.asarray(np.ascontiguousarray(a), dtype=F32)
    return as32(decay), as32(k_w), as32(q_w), chunk_decay


def _alibi_window_bias():
    slopes = np.exp2(-8.0 / B_Q_HEADS * (np.arange(B_Q_HEADS, dtype=np.float64) + 1.0))
    qpos = np.arange(CHUNK)[:, None] + CHUNK
    kpos = np.arange(2 * CHUNK)[None, :]
    dist = (qpos - kpos).astype(np.float64)
    valid = (dist >= 0) & (dist < CHUNK)
    bias = np.where(valid[None], -slopes[:, None, None] * dist[None], -np.inf)
    return jnp.asarray(bias.reshape(B_KV_HEADS, B_Q_PER_KV * CHUNK, 2 * CHUNK), dtype=F32)


def _mixer(x, gain, w_in, b_gate, v_norm, w_s, b_s, sinks, gn, w_a, w_b, w_c, w_o):
    bsz, t_len, _ = x.shape
    decay, k_w, q_w, chunk_decay = _retention_constants()
    bias = _alibi_window_bias()
    bs_cols = jnp.broadcast_to(b_s[:, :, None], (A_GROUPS, CHUNK, CHUNK))
    tok = MIX_TOKENS
    scratch = [
        pltpu.VMEM((tok, A_WIDTH), F32),
        pltpu.VMEM((tok, A_WIDTH), BF16),
        pltpu.VMEM((tok, B_WIDTH), BF16),
        pltpu.VMEM((tok + CHUNK, B_KV_WIDTH), BF16),
        pltpu.VMEM((tok + CHUNK, B_KV_WIDTH), BF16),
        pltpu.VMEM((tok, C_QK_WIDTH), F32),
        pltpu.VMEM((tok, C_QK_WIDTH), F32),
        pltpu.VMEM((tok, C_V_WIDTH), BF16),
        pltpu.VMEM((tok, C_V_WIDTH), F32),
        pltpu.VMEM((tok, A_WIDTH), BF16),
        pltpu.VMEM((tok, B_WIDTH), BF16),
        pltpu.VMEM((tok, C_V_WIDTH), BF16),
        pltpu.VMEM((C_HEADS, C_QK_DIM, C_V_DIM), F32),
    ]
    return pl.pallas_call(
        functools.partial(_mixer_kernel, chunk_decay=chunk_decay),
        grid=(bsz, t_len // tok),
        in_specs=[
            pl.BlockSpec(memory_space=pltpu.SMEM),
            pl.BlockSpec((None, tok, D_MODEL), lambda b, t: (b, t, 0)),
            _resident((1, D_MODEL)),
            _resident((D_MODEL, IN_WIDTH)),
            _resident((1, 3 * D_MODEL)),
            _resident((1, A_WIDTH)),
            _resident((A_GROUPS, CHUNK, CHUNK)),
            _resident((A_GROUPS, CHUNK, CHUNK)),
            _resident((1, C_V_WIDTH)),
            _resident((A_WIDTH, D_MODEL)),
            _resident((B_WIDTH, D_MODEL)),
            _resident((C_V_WIDTH, D_MODEL)),
            _resident((D_MODEL, D_MODEL)),
            _resident((C_HEADS, CHUNK, CHUNK)),
            _resident((C_HEADS, CHUNK, C_QK_DIM)),
            _resident((C_HEADS, CHUNK, C_QK_DIM)),
            _resident((B_KV_HEADS, B_Q_PER_KV * CHUNK, 2 * CHUNK)),
        ],
        out_specs=pl.BlockSpec((None, tok, D_MODEL), lambda b, t: (b, t, 0)),
        out_shape=jax.ShapeDtypeStruct(x.shape, F32),
        scratch_shapes=scratch,
        compiler_params=pltpu.CompilerParams(
            dimension_semantics=("arbitrary", "arbitrary"), vmem_limit_bytes=V7X_VMEM_LIMIT),
        name="mixer",
    )(sinks, x, gain, w_in, b_gate, v_norm, w_s, bs_cols, gn, w_a, w_b, w_c, w_o,
      decay, k_w, q_w, bias)


def kernel(x, ffn1_norm, ffn1_w_in, ffn1_w_out, mix_norm, w_in, b_gate, gmlp_v_norm, gmlp_w_s,
           gmlp_b_s, attn_sinks, ret_gn, w_branch_a, w_branch_b, w_branch_c, w_out,
           ffn2_norm, ffn2_w_in, ffn2_w_out, final_norm):
    bsz, t_len, d = x.shape
    depth = ffn1_norm.shape[0]
    row = lambda v: v.reshape(1, -1)
    fgain = row(final_norm)
    for l in range(depth):
        x2d = _ffn(x.reshape(bsz * t_len, d), row(ffn1_norm[l]), ffn1_w_in[l].astype(BF16),
                   ffn1_w_out[l].astype(BF16), fgain, False)
        x = _mixer(x2d.reshape(bsz, t_len, d), row(mix_norm[l]), w_in[l].astype(BF16),
                   row(b_gate[l]), row(gmlp_v_norm[l]), gmlp_w_s[l], gmlp_b_s[l], attn_sinks[l],
                   row(ret_gn[l]), w_branch_a[l].astype(BF16), w_branch_b[l].astype(BF16),
                   w_branch_c[l].astype(BF16), w_out[l].astype(BF16))
        x2d = _ffn(x.reshape(bsz * t_len, d), row(ffn2_norm[l]), ffn2_w_in[l].astype(BF16),
                   ffn2_w_out[l].astype(BF16), fgain, l == depth - 1)
        x = x2d.reshape(bsz, t_len, d)
    return x
```

```python
import functools
import math

import jax
import jax.numpy as jnp
import numpy as np
from jax import lax
from jax.experimental import pallas as pl
from jax.experimental.pallas import tpu as pltpu

D_MODEL = 1024
D_FF = 2816
CHUNK = 128
A_GROUPS = 4
A_WIDTH = 512
B_Q_HEADS = 8
B_KV_HEADS = 2
B_Q_PER_KV = B_Q_HEADS // B_KV_HEADS
B_HEAD_DIM = 64
B_WIDTH = B_Q_HEADS * B_HEAD_DIM
B_KV_WIDTH = B_KV_HEADS * B_HEAD_DIM
C_HEADS = 4
C_QK_DIM = 128
C_V_DIM = 256
C_QK_WIDTH = C_HEADS * C_QK_DIM
C_V_WIDTH = C_HEADS * C_V_DIM
NORM_EPS = 1e-6
GN_EPS = 1e-5

OFF_AU = 0
OFF_AV = OFF_AU + A_WIDTH
OFF_BQ = OFF_AV + A_WIDTH
OFF_BK = OFF_BQ + B_WIDTH
OFF_BV = OFF_BK + B_KV_WIDTH
OFF_CQ = OFF_BV + B_KV_WIDTH
OFF_CK = OFF_CQ + C_QK_WIDTH
OFF_CV = OFF_CK + C_QK_WIDTH
OFF_CG = OFF_CV + C_V_WIDTH
OFF_GATE = OFF_CG + C_V_WIDTH
IN_WIDTH = OFF_GATE + 3 * D_MODEL

FFN_TOKENS = 512
FFN_COLS = 256
MIX_TOKENS = 256
MIX_CHUNKS = MIX_TOKENS // CHUNK
V7X_VMEM_LIMIT = 56 * 1024 * 1024

BF16 = jnp.bfloat16
F32 = jnp.float32


def _dot(a, b):
    return jnp.dot(a, b, preferred_element_type=F32)


def _dot_nt(a, b):
    return lax.dot_general(a, b, (((1,), (1,)), ((), ())), preferred_element_type=F32)


def _rmsnorm(x, gain):
    return x * lax.rsqrt(jnp.mean(x * x, axis=-1, keepdims=True) + NORM_EPS) * gain


def _resident(shape):
    zeros = (0,) * len(shape)
    return pl.BlockSpec(shape, lambda *_: zeros, pipeline_mode=pl.Buffered(1))


def _ffn_kernel(x_ref, gain_ref, w_in_ref, w_out_ref, fgain_ref, o_ref, *, final_norm):
    x = x_ref[...]
    h = _rmsnorm(x, gain_ref[...]).astype(BF16)
    acc = None
    for c in range(D_FF // FFN_COLS):
        lo = c * FFN_COLS
        a = _dot(h, w_in_ref[:, lo:lo + FFN_COLS])
        b = _dot(h, w_in_ref[:, D_FF + lo:D_FF + lo + FFN_COLS])
        act = (a * jax.nn.sigmoid(a) * b).astype(BF16)
        part = _dot(act, w_out_ref[lo:lo + FFN_COLS, :])
        acc = part if acc is None else acc + part
    y = x + 0.5 * acc
    if final_norm:
        y = _rmsnorm(y, fgain_ref[...])
    o_ref[...] = y


def _ffn(x2d, gain, w_in, w_out, fgain, final_norm):
    n_tok = x2d.shape[0]
    return pl.pallas_call(
        functools.partial(_ffn_kernel, final_norm=final_norm),
        grid=(n_tok // FFN_TOKENS,),
        in_specs=[
            pl.BlockSpec((FFN_TOKENS, D_MODEL), lambda i: (i, 0)),
            _resident((1, D_MODEL)),
            _resident((D_MODEL, 2 * D_FF)),
            _resident((D_FF, D_MODEL)),
            _resident((1, D_MODEL)),
        ],
        out_specs=pl.BlockSpec((FFN_TOKENS, D_MODEL), lambda i: (i, 0)),
        out_shape=jax.ShapeDtypeStruct(x2d.shape, F32),
        compiler_params=pltpu.CompilerParams(
            dimension_semantics=("arbitrary",), vmem_limit_bytes=V7X_VMEM_LIMIT),
        name="ffn_final" if final_norm else "ffn",
    )(x2d, gain, w_in, w_out, fgain)


def _mixer_kernel(sink_ref, x_ref, gain_ref, w_in_ref, bgate_ref, vnorm_ref, ws_ref, bs_ref,
                  gn_ref, wa_ref, wb_ref, wc_ref, wo_ref, decay_ref, kw_ref, qw_ref, bias_ref,
                  o_ref,
                  wsp_s, u_s, v_s, qst_s, k_s, vv_s, cqb_s, cqw_s, ckb_s, ckwt_s, cv_s, sg_s,
                  ya_s, yb_s, yc_s, state_s, stateb_s,
                  *, chunk_decay):
    b = pl.program_id(0)
    t = pl.program_id(1)

    @pl.when(jnp.logical_and(b == 0, t == 0))
    def _():
        row = lax.broadcasted_iota(jnp.int32, (CHUNK, CHUNK), 0)
        col = lax.broadcasted_iota(jnp.int32, (CHUNK, CHUNK), 1)
        for g in range(A_GROUPS):
            wsp_s[g] = jnp.where(row >= col, ws_ref[g], 0.0).astype(BF16)

    @pl.when(t == 0)
    def _():
        state_s[...] = jnp.zeros_like(state_s)
        stateb_s[...] = jnp.zeros_like(stateb_s)
        k_s[:, 0:CHUNK, :] = jnp.zeros((B_KV_HEADS, CHUNK, B_HEAD_DIM), BF16)
        vv_s[:, 0:CHUNK, :] = jnp.zeros((B_KV_HEADS, CHUNK, B_HEAD_DIM), BF16)

    x = x_ref[...]
    h = _rmsnorm(x, gain_ref[...]).astype(BF16)

    def proj(lo, width):
        return _dot(h, w_in_ref[:, lo:lo + width])

    prev_cols = lax.broadcasted_iota(jnp.int32, (1, 2 * CHUNK), 1) < CHUNK
    units_b = [(c, hk) for c in range(MIX_CHUNKS) for hk in range(B_KV_HEADS)]
    units_c = [(c, hd) for c in range(MIX_CHUNKS) for hd in range(C_HEADS)]


    qb = (proj(OFF_BQ, B_WIDTH) * (1.0 / math.sqrt(B_HEAD_DIM))).astype(BF16)
    kb = proj(OFF_BK, B_KV_WIDTH).astype(BF16)
    vb = proj(OFF_BV, B_KV_WIDTH).astype(BF16)
    for hk in range(B_KV_HEADS):
        kcols = slice(hk * B_HEAD_DIM, (hk + 1) * B_HEAD_DIM)
        k_s[hk, CHUNK:, :] = kb[:, kcols]
        vv_s[hk, CHUNK:, :] = vb[:, kcols]
        for c in range(MIX_CHUNKS):
            qst_s[c * B_KV_HEADS + hk] = jnp.concatenate(
                [qb[c * CHUNK:(c + 1) * CHUNK, j * B_HEAD_DIM:(j + 1) * B_HEAD_DIM]
                 for j in range(hk * B_Q_PER_KV, (hk + 1) * B_Q_PER_KV)], axis=0)

    def attention_probs(c, hk):
        k2 = k_s[hk, c * CHUNK:(c + 2) * CHUNK, :]
        scores = _dot_nt(qst_s[c * B_KV_HEADS + hk], k2) + bias_ref[hk]
        if c == 0:
            scores = jnp.where(jnp.logical_and(t == 0, prev_cols), -jnp.inf, scores)
        probs, denoms = [], []
        for g in range(B_Q_PER_KV):
            sc = scores[g * CHUNK:(g + 1) * CHUNK]
            sink = sink_ref[hk * B_Q_PER_KV + g]
            m = jnp.maximum(jnp.max(sc, axis=-1, keepdims=True), sink)
            p = jnp.exp(sc - m)
            denoms.append(jnp.sum(p, axis=-1, keepdims=True) + jnp.exp(sink - m))
            probs.append(p.astype(BF16))
        return jnp.concatenate(probs, axis=0), denoms

    probs_b = {u: attention_probs(*u) for u in units_b}

    z = jax.nn.gelu(proj(OFF_AU, 2 * A_WIDTH))
    u_s[...] = z[:, :A_WIDTH]
    v_s[...] = _rmsnorm(z[:, A_WIDTH:], vnorm_ref[...]).astype(BF16)
    cq = proj(OFF_CQ, C_QK_WIDTH)
    cqb_s[...] = cq.astype(BF16)
    cqw_s[...] = (cq * qw_ref[...]).astype(BF16)
    ck = proj(OFF_CK, C_QK_WIDTH) * (C_QK_DIM ** -0.5)
    ckb_s[...] = ck.astype(BF16)
    ckwt_s[...] = (ck * kw_ref[...]).T.astype(BF16)
    cv_s[...] = proj(OFF_CV, C_V_WIDTH).astype(BF16)
    cg = proj(OFF_CG, C_V_WIDTH)
    sg_s[...] = cg * jax.nn.sigmoid(cg)

    for c, hk in units_b:
        rows = slice(c * CHUNK, (c + 1) * CHUNK)
        probs, denoms = probs_b[(c, hk)]
        out = _dot(probs, vv_s[hk, c * CHUNK:(c + 2) * CHUNK, :])
        for g in range(B_Q_PER_KV):
            j = hk * B_Q_PER_KV + g
            yb_s[rows, j * B_HEAD_DIM:(j + 1) * B_HEAD_DIM] = (
                out[g * CHUNK:(g + 1) * CHUNK] / denoms[g]).astype(BF16)

    for c in range(MIX_CHUNKS):
        rows = slice(c * CHUNK, (c + 1) * CHUNK)
        for g in range(A_GROUPS):
            cols = slice(g * CHUNK, (g + 1) * CHUNK)
            s = _dot(wsp_s[g], v_s[rows, cols]) + bs_ref[g]
            ya_s[rows, cols] = (u_s[rows, cols] * s).astype(BF16)

    def retention_scores(c, hd):
        rows = slice(c * CHUNK, (c + 1) * CHUNK)
        qk = slice(hd * C_QK_DIM, (hd + 1) * C_QK_DIM)
        return (_dot_nt(cqb_s[rows, qk], ckb_s[rows, qk]) * decay_ref[hd]).astype(BF16)

    scores_c = {u: retention_scores(*u) for u in units_c}

    for c, hd in units_c:
        rows = slice(c * CHUNK, (c + 1) * CHUNK)
        qk = slice(hd * C_QK_DIM, (hd + 1) * C_QK_DIM)
        vc = slice(hd * C_V_DIM, (hd + 1) * C_V_DIM)
        vh = cv_s[rows, vc]
        lhs = jnp.concatenate([scores_c[(c, hd)], cqw_s[rows, qk]], axis=1)
        rhs = jnp.concatenate([vh, stateb_s[hd]], axis=0)
        o = _dot(lhs, rhs)
        mu = jnp.mean(o, axis=-1, keepdims=True)
        var = jnp.mean(jnp.square(o - mu), axis=-1, keepdims=True)
        o = (o - mu) * lax.rsqrt(var + GN_EPS) * gn_ref[:, vc]
        yc_s[rows, vc] = (o * sg_s[rows, vc]).astype(BF16)
        state = chunk_decay[hd] * state_s[hd] + _dot(ckwt_s[qk, rows], vh)
        state_s[hd] = state
        stateb_s[hd] = state.astype(BF16)

    k_s[:, 0:CHUNK, :] = k_s[:, MIX_TOKENS:, :]
    vv_s[:, 0:CHUNK, :] = vv_s[:, MIX_TOKENS:, :]

    merged = None
    for j, (y_s, w_ref) in enumerate(((ya_s, wa_ref), (yb_s, wb_ref), (yc_s, wc_ref))):
        lo = j * D_MODEL
        gate = jax.nn.sigmoid(proj(OFF_GATE + lo, D_MODEL) + bgate_ref[:, lo:lo + D_MODEL])
        term = gate * _dot(y_s[...], w_ref[...])
        merged = term if merged is None else merged + term
    o_ref[...] = x + _dot(merged.astype(BF16), wo_ref[...])


def _retention_constants():
    gamma = 1.0 - np.exp2(-5.0 - np.arange(C_HEADS, dtype=np.float64))
    log_gamma = np.log(gamma)
    idx = np.arange(CHUNK, dtype=np.float64)
    dist = idx[:, None] - idx[None, :]
    decay = np.where(dist[None] >= 0, np.exp(log_gamma[:, None, None] * np.maximum(dist, 0.0)[None]), 0.0)
    k_w = np.exp(log_gamma[:, None] * (CHUNK - 1.0 - idx)[None, :])
    q_w = np.exp(log_gamma[:, None] * (idx + 1.0)[None, :])

    def per_token_cols(w):
        cols = np.repeat(w.T, C_QK_DIM, axis=1)
        return np.tile(cols, (MIX_CHUNKS, 1))

    chunk_decay = tuple(float(v) for v in np.exp(log_gamma * CHUNK))
    as32 = lambda a: jnp.asarray(np.ascontiguousarray(a), dtype=F32)
    return as32(decay), as32(per_token_cols(k_w)), as32(per_token_cols(q_w)), chunk_decay


def _alibi_window_bias():
    slopes = np.exp2(-8.0 / B_Q_HEADS * (np.arange(B_Q_HEADS, dtype=np.float64) + 1.0))
    qpos = np.arange(CHUNK)[:, None] + CHUNK
    kpos = np.arange(2 * CHUNK)[None, :]
    dist = (qpos - kpos).astype(np.float64)
    valid = (dist >= 0) & (dist < CHUNK)
    bias = np.where(valid[None], -slopes[:, None, None] * dist[None], -np.inf)
    return jnp.asarray(bias.reshape(B_KV_HEADS, B_Q_PER_KV * CHUNK, 2 * CHUNK), dtype=F32)


def _mixer(x, gain, w_in, b_gate, v_norm, w_s, b_s, sinks, gn, w_a, w_b, w_c, w_o):
    bsz, t_len, _ = x.shape
    decay, k_w, q_w, chunk_decay = _retention_constants()
    bias = _alibi_window_bias()
    bs_cols = jnp.broadcast_to(b_s[:, :, None], (A_GROUPS, CHUNK, CHUNK))
    tok = MIX_TOKENS
    scratch = [
        pltpu.VMEM((A_GROUPS, CHUNK, CHUNK), BF16),
        pltpu.VMEM((tok, A_WIDTH), F32),
        pltpu.VMEM((tok, A_WIDTH), BF16),
        pltpu.VMEM((MIX_CHUNKS * B_KV_HEADS, B_Q_PER_KV * CHUNK, B_HEAD_DIM), BF16),
        pltpu.VMEM((B_KV_HEADS, tok + CHUNK, B_HEAD_DIM), BF16),
        pltpu.VMEM((B_KV_HEADS, tok + CHUNK, B_HEAD_DIM), BF16),
        pltpu.VMEM((tok, C_QK_WIDTH), BF16),
        pltpu.VMEM((tok, C_QK_WIDTH), BF16),
        pltpu.VMEM((tok, C_QK_WIDTH), BF16),
        pltpu.VMEM((C_QK_WIDTH, tok), BF16),
        pltpu.VMEM((tok, C_V_WIDTH), BF16),
        pltpu.VMEM((tok, C_V_WIDTH), F32),
        pltpu.VMEM((tok, A_WIDTH), BF16),
        pltpu.VMEM((tok, B_WIDTH), BF16),
        pltpu.VMEM((tok, C_V_WIDTH), BF16),
        pltpu.VMEM((C_HEADS, C_QK_DIM, C_V_DIM), F32),
        pltpu.VMEM((C_HEADS, C_QK_DIM, C_V_DIM), BF16),
    ]
    return pl.pallas_call(
        functools.partial(_mixer_kernel, chunk_decay=chunk_decay),
        grid=(bsz, t_len // tok),
        in_specs=[
            pl.BlockSpec(memory_space=pltpu.SMEM),
            pl.BlockSpec((None, tok, D_MODEL), lambda b, t: (b, t, 0)),
            _resident((1, D_MODEL)),
            _resident((D_MODEL, IN_WIDTH)),
            _resident((1, 3 * D_MODEL)),
            _resident((1, A_WIDTH)),
            _resident((A_GROUPS, CHUNK, CHUNK)),
            _resident((A_GROUPS, CHUNK, CHUNK)),
            _resident((1, C_V_WIDTH)),
            _resident((A_WIDTH, D_MODEL)),
            _resident((B_WIDTH, D_MODEL)),
            _resident((C_V_WIDTH, D_MODEL)),
            _resident((D_MODEL, D_MODEL)),
            _resident((C_HEADS, CHUNK, CHUNK)),
            _resident((tok, C_QK_WIDTH)),
            _resident((tok, C_QK_WIDTH)),
            _resident((B_KV_HEADS, B_Q_PER_KV * CHUNK, 2 * CHUNK)),
        ],
        out_specs=pl.BlockSpec((None, tok, D_MODEL), lambda b, t: (b, t, 0)),
        out_shape=jax.ShapeDtypeStruct(x.shape, F32),
        scratch_shapes=scratch,
        compiler_params=pltpu.CompilerParams(
            dimension_semantics=("arbitrary", "arbitrary"), vmem_limit_bytes=V7X_VMEM_LIMIT),
        name="mixer",
    )(sinks, x, gain, w_in, b_gate, v_norm, w_s, bs_cols, gn, w_a, w_b, w_c, w_o,
      decay, k_w, q_w, bias)


def kernel(x, ffn1_norm, ffn1_w_in, ffn1_w_out, mix_norm, w_in, b_gate, gmlp_v_norm, gmlp_w_s,
           gmlp_b_s, attn_sinks, ret_gn, w_branch_a, w_branch_b, w_branch_c, w_out,
           ffn2_norm, ffn2_w_in, ffn2_w_out, final_norm):
    bsz, t_len, d = x.shape
    depth = ffn1_norm.shape[0]
    row = lambda v: v.reshape(1, -1)
    fgain = row(final_norm)
    for l in range(depth):
        x2d = _ffn(x.reshape(bsz * t_len, d), row(ffn1_norm[l]), ffn1_w_in[l].astype(BF16),
                   ffn1_w_out[l].astype(BF16), fgain, False)
        x = _mixer(x2d.reshape(bsz, t_len, d), row(mix_norm[l]), w_in[l].astype(BF16),
                   row(b_gate[l]), row(gmlp_v_norm[l]), gmlp_w_s[l], gmlp_b_s[l], attn_sinks[l],
                   row(ret_gn[l]), w_branch_a[l].astype(BF16), w_branch_b[l].astype(BF16),
                   w_branch_c[l].astype(BF16), w_out[l].astype(BF16))
        x2d = _ffn(x.reshape(bsz * t_len, d), row(ffn2_norm[l]), ffn2_w_in[l].astype(BF16),
                   ffn2_w_out[l].astype(BF16), fgain, l == depth - 1)
        x = x2d.reshape(bsz, t_len, d)
    return x
```

```python
import functools
import math

import jax
import jax.numpy as jnp
import numpy as np
from jax import lax
from jax.experimental import pallas as pl
from jax.experimental.pallas import tpu as pltpu

D_MODEL = 1024
D_FF = 2816
CHUNK = 128
A_GROUPS = 4
A_WIDTH = 512
B_Q_HEADS = 8
B_KV_HEADS = 2
B_Q_PER_KV = B_Q_HEADS // B_KV_HEADS
B_HEAD_DIM = 64
B_WIDTH = B_Q_HEADS * B_HEAD_DIM
B_KV_WIDTH = B_KV_HEADS * B_HEAD_DIM
C_HEADS = 4
C_QK_DIM = 128
C_V_DIM = 256
C_QK_WIDTH = C_HEADS * C_QK_DIM
C_V_WIDTH = C_HEADS * C_V_DIM
NORM_EPS = 1e-6
GN_EPS = 1e-5

OFF_AU = 0
OFF_AV = OFF_AU + A_WIDTH
OFF_BQ = OFF_AV + A_WIDTH
OFF_BK = OFF_BQ + B_WIDTH
OFF_BV = OFF_BK + B_KV_WIDTH
OFF_CQ = OFF_BV + B_KV_WIDTH
OFF_CK = OFF_CQ + C_QK_WIDTH
OFF_CV = OFF_CK + C_QK_WIDTH
OFF_CG = OFF_CV + C_V_WIDTH
OFF_GATE = OFF_CG + C_V_WIDTH
IN_WIDTH = OFF_GATE + 3 * D_MODEL

FFN_TOKENS = 1024
FFN_SUBTILES = 2
FFN_ROWS = FFN_TOKENS // FFN_SUBTILES
FFN_COLS = 256
MIX_TOKENS = 512
MIX_SUBTILES = 2
SUB_TOKENS = MIX_TOKENS // MIX_SUBTILES
SUB_CHUNKS = SUB_TOKENS // CHUNK
MIX_CHUNKS = MIX_TOKENS // CHUNK
V7X_VMEM_LIMIT = 56 * 1024 * 1024

BF16 = jnp.bfloat16
F32 = jnp.float32


def _dot(a, b):
    return jnp.dot(a, b, preferred_element_type=F32)


def _dot_nt(a, b):
    return lax.dot_general(a, b, (((1,), (1,)), ((), ())), preferred_element_type=F32)


def _rmsnorm(x, gain):
    return x * lax.rsqrt(jnp.mean(x * x, axis=-1, keepdims=True) + NORM_EPS) * gain


def _resident(shape):
    zeros = (0,) * len(shape)
    return pl.BlockSpec(shape, lambda *_: zeros, pipeline_mode=pl.Buffered(1))


def _ffn_kernel(x_ref, gain_ref, w_in_ref, w_out_ref, fgain_ref, o_ref, *, final_norm):
    for s in range(FFN_SUBTILES):
        rows = slice(s * FFN_ROWS, (s + 1) * FFN_ROWS)
        x = x_ref[rows, :]
        h = _rmsnorm(x, gain_ref[...]).astype(BF16)
        acc = None
        for c in range(D_FF // FFN_COLS):
            lo = c * FFN_COLS
            a = _dot(h, w_in_ref[:, lo:lo + FFN_COLS])
            b = _dot(h, w_in_ref[:, D_FF + lo:D_FF + lo + FFN_COLS])
            act = (a * jax.nn.sigmoid(a) * b).astype(BF16)
            part = _dot(act, w_out_ref[lo:lo + FFN_COLS, :])
            acc = part if acc is None else acc + part
        y = x + 0.5 * acc
        if final_norm:
            y = _rmsnorm(y, fgain_ref[...])
        o_ref[rows, :] = y


def _ffn(x2d, gain, w_in, w_out, fgain, final_norm):
    n_tok = x2d.shape[0]
    return pl.pallas_call(
        functools.partial(_ffn_kernel, final_norm=final_norm),
        grid=(n_tok // FFN_TOKENS,),
        in_specs=[
            pl.BlockSpec((FFN_TOKENS, D_MODEL), lambda i: (i, 0)),
            _resident((1, D_MODEL)),
            _resident((D_MODEL, 2 * D_FF)),
            _resident((D_FF, D_MODEL)),
            _resident((1, D_MODEL)),
        ],
        out_specs=pl.BlockSpec((FFN_TOKENS, D_MODEL), lambda i: (i, 0)),
        out_shape=jax.ShapeDtypeStruct(x2d.shape, F32),
        compiler_params=pltpu.CompilerParams(
            dimension_semantics=("arbitrary",), vmem_limit_bytes=V7X_VMEM_LIMIT),
        name="ffn_final" if final_norm else "ffn",
    )(x2d, gain, w_in, w_out, fgain)


def _mixer_kernel(sink_ref, x_ref, gain_ref, w_in_ref, bgate_ref, vnorm_ref, ws_ref, bs_ref,
                  gn_ref, wa_ref, wb_ref, wc_ref, wo_ref, decay_ref, kw_ref, qw_ref, bias_ref,
                  o_ref,
                  wsp_s, u_s, v_s, qst_s, k_s, vv_s, cqb_s, cqw_s, ckb_s, ckwt_s, cv_s, sg_s,
                  ya_s, yb_s, yc_s, state_s, stateb_s,
                  *, chunk_decay):
    b = pl.program_id(0)
    t = pl.program_id(1)

    @pl.when(jnp.logical_and(b == 0, t == 0))
    def _():
        row = lax.broadcasted_iota(jnp.int32, (CHUNK, CHUNK), 0)
        col = lax.broadcasted_iota(jnp.int32, (CHUNK, CHUNK), 1)
        for g in range(A_GROUPS):
            wsp_s[g] = jnp.where(row >= col, ws_ref[g], 0.0).astype(BF16)

    @pl.when(t == 0)
    def _():
        state_s[...] = jnp.zeros_like(state_s)
        stateb_s[...] = jnp.zeros_like(stateb_s)
        k_s[:, 0:CHUNK, :] = jnp.zeros((B_KV_HEADS, CHUNK, B_HEAD_DIM), BF16)
        vv_s[:, 0:CHUNK, :] = jnp.zeros((B_KV_HEADS, CHUNK, B_HEAD_DIM), BF16)

    def sub_tile(s):
        r0 = s * SUB_TOKENS
        c0 = s * SUB_CHUNKS
        sub = slice(r0, r0 + SUB_TOKENS)
        x = x_ref[sub, :]
        h = _rmsnorm(x, gain_ref[...]).astype(BF16)

        def proj(lo, width):
            return _dot(h, w_in_ref[:, lo:lo + width])

        def chunk_rows(c):
            return slice((c0 + c) * CHUNK, (c0 + c + 1) * CHUNK)

        def window_rows(c):
            return slice((c0 + c) * CHUNK, (c0 + c + 2) * CHUNK)

        prev_cols = lax.broadcasted_iota(jnp.int32, (1, 2 * CHUNK), 1) < CHUNK
        units_b = [(c, hk) for c in range(SUB_CHUNKS) for hk in range(B_KV_HEADS)]
        units_c = [(c, hd) for c in range(SUB_CHUNKS) for hd in range(C_HEADS)]


        qb = (proj(OFF_BQ, B_WIDTH) * (1.0 / math.sqrt(B_HEAD_DIM))).astype(BF16)
        kb = proj(OFF_BK, B_KV_WIDTH).astype(BF16)
        vb = proj(OFF_BV, B_KV_WIDTH).astype(BF16)
        for hk in range(B_KV_HEADS):
            kcols = slice(hk * B_HEAD_DIM, (hk + 1) * B_HEAD_DIM)
            k_s[hk, CHUNK + r0:CHUNK + r0 + SUB_TOKENS, :] = kb[:, kcols]
            vv_s[hk, CHUNK + r0:CHUNK + r0 + SUB_TOKENS, :] = vb[:, kcols]
            for c in range(SUB_CHUNKS):
                qst_s[(c0 + c) * B_KV_HEADS + hk] = jnp.concatenate(
                    [qb[c * CHUNK:(c + 1) * CHUNK, j * B_HEAD_DIM:(j + 1) * B_HEAD_DIM]
                     for j in range(hk * B_Q_PER_KV, (hk + 1) * B_Q_PER_KV)], axis=0)

        def attention_probs(c, hk):
            k2 = k_s[hk, window_rows(c), :]
            scores = _dot_nt(qst_s[(c0 + c) * B_KV_HEADS + hk], k2) + bias_ref[hk]
            if c0 + c == 0:
                scores = jnp.where(jnp.logical_and(t == 0, prev_cols), -jnp.inf, scores)
            probs, denoms = [], []
            for g in range(B_Q_PER_KV):
                sc = scores[g * CHUNK:(g + 1) * CHUNK]
                sink = sink_ref[hk * B_Q_PER_KV + g]
                m = jnp.maximum(jnp.max(sc, axis=-1, keepdims=True), sink)
                p = jnp.exp(sc - m)
                denoms.append(jnp.sum(p, axis=-1, keepdims=True) + jnp.exp(sink - m))
                probs.append(p.astype(BF16))
            return jnp.concatenate(probs, axis=0), denoms

        probs_b = {u: attention_probs(*u) for u in units_b}

        z = jax.nn.gelu(proj(OFF_AU, 2 * A_WIDTH))
        u_s[sub, :] = z[:, :A_WIDTH]
        v_s[sub, :] = _rmsnorm(z[:, A_WIDTH:], vnorm_ref[...]).astype(BF16)
        cq = proj(OFF_CQ, C_QK_WIDTH)
        cqb_s[sub, :] = cq.astype(BF16)
        cqw_s[sub, :] = (cq * qw_ref[...]).astype(BF16)
        ck = proj(OFF_CK, C_QK_WIDTH) * (C_QK_DIM ** -0.5)
        ckb_s[sub, :] = ck.astype(BF16)
        ckwt_s[:, sub] = (ck * kw_ref[...]).T.astype(BF16)
        cv_s[sub, :] = proj(OFF_CV, C_V_WIDTH).astype(BF16)
        cg = proj(OFF_CG, C_V_WIDTH)
        sg_s[sub, :] = cg * jax.nn.sigmoid(cg)

        for c, hk in units_b:
            probs, denoms = probs_b[(c, hk)]
            out = _dot(probs, vv_s[hk, window_rows(c), :])
            for g in range(B_Q_PER_KV):
                j = hk * B_Q_PER_KV + g
                yb_s[chunk_rows(c), j * B_HEAD_DIM:(j + 1) * B_HEAD_DIM] = (
                    out[g * CHUNK:(g + 1) * CHUNK] / denoms[g]).astype(BF16)

        for c in range(SUB_CHUNKS):
            rows = chunk_rows(c)
            for g in range(A_GROUPS):
                cols = slice(g * CHUNK, (g + 1) * CHUNK)
                mixed = _dot(wsp_s[g], v_s[rows, cols]) + bs_ref[g]
                ya_s[rows, cols] = (u_s[rows, cols] * mixed).astype(BF16)

        def retention_scores(c, hd):
            rows = chunk_rows(c)
            qk = slice(hd * C_QK_DIM, (hd + 1) * C_QK_DIM)
            return (_dot_nt(cqb_s[rows, qk], ckb_s[rows, qk]) * decay_ref[hd]).astype(BF16)

        scores_c = {u: retention_scores(*u) for u in units_c}

        for c, hd in units_c:
            rows = chunk_rows(c)
            qk = slice(hd * C_QK_DIM, (hd + 1) * C_QK_DIM)
            vc = slice(hd * C_V_DIM, (hd + 1) * C_V_DIM)
            vh = cv_s[rows, vc]
            lhs = jnp.concatenate([scores_c[(c, hd)], cqw_s[rows, qk]], axis=1)
            rhs = jnp.concatenate([vh, stateb_s[hd]], axis=0)
            o = _dot(lhs, rhs)
            mu = jnp.mean(o, axis=-1, keepdims=True)
            var = jnp.mean(jnp.square(o - mu), axis=-1, keepdims=True)
            o = (o - mu) * lax.rsqrt(var + GN_EPS) * gn_ref[:, vc]
            yc_s[rows, vc] = (o * sg_s[rows, vc]).astype(BF16)
            state = chunk_decay[hd] * state_s[hd] + _dot(ckwt_s[qk, rows], vh)
            state_s[hd] = state
            stateb_s[hd] = state.astype(BF16)

        merged = None
        for j, (y_s, w_ref) in enumerate(((ya_s, wa_ref), (yb_s, wb_ref), (yc_s, wc_ref))):
            lo = j * D_MODEL
            gate = jax.nn.sigmoid(proj(OFF_GATE + lo, D_MODEL) + bgate_ref[:, lo:lo + D_MODEL])
            term = gate * _dot(y_s[sub, :], w_ref[...])
            merged = term if merged is None else merged + term
        o_ref[sub, :] = x + _dot(merged.astype(BF16), wo_ref[...])

    for s in range(MIX_SUBTILES):
        sub_tile(s)

    k_s[:, 0:CHUNK, :] = k_s[:, MIX_TOKENS:, :]
    vv_s[:, 0:CHUNK, :] = vv_s[:, MIX_TOKENS:, :]


def _retention_constants():
    gamma = 1.0 - np.exp2(-5.0 - np.arange(C_HEADS, dtype=np.float64))
    log_gamma = np.log(gamma)
    idx = np.arange(CHUNK, dtype=np.float64)
    dist = idx[:, None] - idx[None, :]
    decay = np.where(dist[None] >= 0, np.exp(log_gamma[:, None, None] * np.maximum(dist, 0.0)[None]), 0.0)
    k_w = np.exp(log_gamma[:, None] * (CHUNK - 1.0 - idx)[None, :])
    q_w = np.exp(log_gamma[:, None] * (idx + 1.0)[None, :])

    def per_token_cols(w):
        cols = np.repeat(w.T, C_QK_DIM, axis=1)
        return np.tile(cols, (SUB_CHUNKS, 1))

    chunk_decay = tuple(float(v) for v in np.exp(log_gamma * CHUNK))
    as32 = lambda a: jnp.asarray(np.ascontiguousarray(a), dtype=F32)
    return as32(decay), as32(per_token_cols(k_w)), as32(per_token_cols(q_w)), chunk_decay


def _alibi_window_bias():
    slopes = np.exp2(-8.0 / B_Q_HEADS * (np.arange(B_Q_HEADS, dtype=np.float64) + 1.0))
    qpos = np.arange(CHUNK)[:, None] + CHUNK
    kpos = np.arange(2 * CHUNK)[None, :]
    dist = (qpos - kpos).astype(np.float64)
    valid = (dist >= 0) & (dist < CHUNK)
    bias = np.where(valid[None], -slopes[:, None, None] * dist[None], -np.inf)
    return jnp.asarray(bias.reshape(B_KV_HEADS, B_Q_PER_KV * CHUNK, 2 * CHUNK), dtype=F32)


def _mixer(x, gain, w_in, b_gate, v_norm, w_s, b_s, sinks, gn, w_a, w_b, w_c, w_o):
    bsz, t_len, _ = x.shape
    decay, k_w, q_w, chunk_decay = _retention_constants()
    bias = _alibi_window_bias()
    bs_cols = jnp.broadcast_to(b_s[:, :, None], (A_GROUPS, CHUNK, CHUNK))
    tok = MIX_TOKENS
    scratch = [
        pltpu.VMEM((A_GROUPS, CHUNK, CHUNK), BF16),
        pltpu.VMEM((tok, A_WIDTH), F32),
        pltpu.VMEM((tok, A_WIDTH), BF16),
        pltpu.VMEM((MIX_CHUNKS * B_KV_HEADS, B_Q_PER_KV * CHUNK, B_HEAD_DIM), BF16),
        pltpu.VMEM((B_KV_HEADS, tok + CHUNK, B_HEAD_DIM), BF16),
        pltpu.VMEM((B_KV_HEADS, tok + CHUNK, B_HEAD_DIM), BF16),
        pltpu.VMEM((tok, C_QK_WIDTH), BF16),
        pltpu.VMEM((tok, C_QK_WIDTH), BF16),
        pltpu.VMEM((tok, C_QK_WIDTH), BF16),
        pltpu.VMEM((C_QK_WIDTH, tok), BF16),
        pltpu.VMEM((tok, C_V_WIDTH), BF16),
        pltpu.VMEM((tok, C_V_WIDTH), F32),
        pltpu.VMEM((tok, A_WIDTH), BF16),
        pltpu.VMEM((tok, B_WIDTH), BF16),
        pltpu.VMEM((tok, C_V_WIDTH), BF16),
        pltpu.VMEM((C_HEADS, C_QK_DIM, C_V_DIM), F32),
        pltpu.VMEM((C_HEADS, C_QK_DIM, C_V_DIM), BF16),
    ]
    return pl.pallas_call(
        functools.partial(_mixer_kernel, chunk_decay=chunk_decay),
        grid=(bsz, t_len // tok),
        in_specs=[
            pl.BlockSpec(memory_space=pltpu.SMEM),
            pl.BlockSpec((None, tok, D_MODEL), lambda b, t: (b, t, 0)),
            _resident((1, D_MODEL)),
            _resident((D_MODEL, IN_WIDTH)),
            _resident((1, 3 * D_MODEL)),
            _resident((1, A_WIDTH)),
            _resident((A_GROUPS, CHUNK, CHUNK)),
            _resident((A_GROUPS, CHUNK, CHUNK)),
            _resident((1, C_V_WIDTH)),
            _resident((A_WIDTH, D_MODEL)),
            _resident((B_WIDTH, D_MODEL)),
            _resident((C_V_WIDTH, D_MODEL)),
            _resident((D_MODEL, D_MODEL)),
            _resident((C_HEADS, CHUNK, CHUNK)),
            _resident((SUB_TOKENS, C_QK_WIDTH)),
            _resident((SUB_TOKENS, C_QK_WIDTH)),
            _resident((B_KV_HEADS, B_Q_PER_KV * CHUNK, 2 * CHUNK)),
        ],
        out_specs=pl.BlockSpec((None, tok, D_MODEL), lambda b, t: (b, t, 0)),
        out_shape=jax.ShapeDtypeStruct(x.shape, F32),
        scratch_shapes=scratch,
        compiler_params=pltpu.CompilerParams(
            dimension_semantics=("arbitrary", "arbitrary"), vmem_limit_bytes=V7X_VMEM_LIMIT),
        name="mixer",
    )(sinks, x, gain, w_in, b_gate, v_norm, w_s, bs_cols, gn, w_a, w_b, w_c, w_o,
      decay, k_w, q_w, bias)


def kernel(x, ffn1_norm, ffn1_w_in, ffn1_w_out, mix_norm, w_in, b_gate, gmlp_v_norm, gmlp_w_s,
           gmlp_b_s, attn_sinks, ret_gn, w_branch_a, w_branch_b, w_branch_c, w_out,
           ffn2_norm, ffn2_w_in, ffn2_w_out, final_norm):
    bsz, t_len, d = x.shape
    depth = ffn1_norm.shape[0]
    row = lambda v: v.reshape(1, -1)
    fgain = row(final_norm)
    for l in range(depth):
        x2d = _ffn(x.reshape(bsz * t_len, d), row(ffn1_norm[l]), ffn1_w_in[l].astype(BF16),
                   ffn1_w_out[l].astype(BF16), fgain, False)
        x = _mixer(x2d.reshape(bsz, t_len, d), row(mix_norm[l]), w_in[l].astype(BF16),
                   row(b_gate[l]), row(gmlp_v_norm[l]), gmlp_w_s[l], gmlp_b_s[l], attn_sinks[l],
                   row(ret_gn[l]), w_branch_a[l].astype(BF16), w_branch_b[l].astype(BF16),
                   w_branch_c[l].astype(BF16), w_out[l].astype(BF16))
        x2d = _ffn(x.reshape(bsz * t_len, d), row(ffn2_norm[l]), ffn2_w_in[l].astype(BF16),
                   ffn2_w_out[l].astype(BF16), fgain, l == depth - 1)
        x = x2d.reshape(bsz, t_len, d)
    return x
```

```python
import functools
import math

import jax
import jax.numpy as jnp
import numpy as np
from jax import lax
from jax.experimental import pallas as pl
from jax.experimental.pallas import tpu as pltpu

D_MODEL = 1024
D_FF = 2816
CHUNK = 128
A_GROUPS = 4
A_WIDTH = 512
B_Q_HEADS = 8
B_KV_HEADS = 2
B_Q_PER_KV = B_Q_HEADS // B_KV_HEADS
B_HEAD_DIM = 64
B_WIDTH = B_Q_HEADS * B_HEAD_DIM
B_KV_WIDTH = B_KV_HEADS * B_HEAD_DIM
C_HEADS = 4
C_QK_DIM = 128
C_V_DIM = 256
C_QK_WIDTH = C_HEADS * C_QK_DIM
C_V_WIDTH = C_HEADS * C_V_DIM
NORM_EPS = 1e-6
GN_EPS = 1e-5

OFF_AU = 0
OFF_AV = OFF_AU + A_WIDTH
OFF_BQ = OFF_AV + A_WIDTH
OFF_BK = OFF_BQ + B_WIDTH
OFF_BV = OFF_BK + B_KV_WIDTH
OFF_CQ = OFF_BV + B_KV_WIDTH
OFF_CK = OFF_CQ + C_QK_WIDTH
OFF_CV = OFF_CK + C_QK_WIDTH
OFF_CG = OFF_CV + C_V_WIDTH
OFF_GATE = OFF_CG + C_V_WIDTH
IN_WIDTH = OFF_GATE + 3 * D_MODEL

FFN_TOKENS = 1024
FFN_SUBTILES = 2
FFN_ROWS = FFN_TOKENS // FFN_SUBTILES
FFN_COLS = 256
MIX_TOKENS = 512
MIX_SUBTILES = 2
SUB_TOKENS = MIX_TOKENS // MIX_SUBTILES
SUB_CHUNKS = SUB_TOKENS // CHUNK
MIX_CHUNKS = MIX_TOKENS // CHUNK
CAST_STEPS = 8
BF16_SUBLANES = 16
V7X_VMEM_LIMIT = 56 * 1024 * 1024

BF16 = jnp.bfloat16
F32 = jnp.float32


def _dot(a, b):
    return jnp.dot(a, b, preferred_element_type=F32)


def _dot_nt(a, b):
    return lax.dot_general(a, b, (((1,), (1,)), ((), ())), preferred_element_type=F32)


def _rmsnorm(x, gain):
    return x * lax.rsqrt(jnp.mean(x * x, axis=-1, keepdims=True) + NORM_EPS) * gain


def _resident(shape, layer=None):
    zeros = (0,) * len(shape)
    if layer is None:
        return pl.BlockSpec(shape, lambda *_: zeros, pipeline_mode=pl.Buffered(1))
    return pl.BlockSpec((None,) + shape, lambda *_: (layer,) + zeros, pipeline_mode=pl.Buffered(1))


def _cast_kernel(*refs):
    n = len(refs) // 2
    for src, dst in zip(refs[:n], refs[n:]):
        dst[...] = src[...].astype(BF16)


def _weights_to_bf16(weights):
    depth = weights[0].shape[0]
    specs = []
    for w in weights:
        rows = w.shape[1] // CAST_STEPS
        assert w.shape[0] == depth and rows * CAST_STEPS == w.shape[1] and rows % BF16_SUBLANES == 0
        specs.append(pl.BlockSpec((None, rows, w.shape[2]), lambda l, i: (l, i, 0)))
    return pl.pallas_call(
        _cast_kernel,
        grid=(depth, CAST_STEPS),
        in_specs=specs,
        out_specs=specs,
        out_shape=[jax.ShapeDtypeStruct(w.shape, BF16) for w in weights],
        compiler_params=pltpu.CompilerParams(
            dimension_semantics=("arbitrary", "arbitrary"), vmem_limit_bytes=V7X_VMEM_LIMIT),
        name="weights_to_bf16",
    )(*weights)


def _ffn_kernel(x_ref, gain_ref, w_in_ref, w_out_ref, fgain_ref, o_ref, *, final_norm):
    for s in range(FFN_SUBTILES):
        rows = slice(s * FFN_ROWS, (s + 1) * FFN_ROWS)
        x = x_ref[rows, :]
        h = _rmsnorm(x, gain_ref[...]).astype(BF16)
        acc = None
        for c in range(D_FF // FFN_COLS):
            lo = c * FFN_COLS
            a = _dot(h, w_in_ref[:, lo:lo + FFN_COLS])
            b = _dot(h, w_in_ref[:, D_FF + lo:D_FF + lo + FFN_COLS])
            act = (a * jax.nn.sigmoid(a) * b).astype(BF16)
            part = _dot(act, w_out_ref[lo:lo + FFN_COLS, :])
            acc = part if acc is None else acc + part
        y = x + 0.5 * acc
        if final_norm:
            y = _rmsnorm(y, fgain_ref[...])
        o_ref[rows, :] = y


def _ffn(x2d, gain, w_in, w_out, layer, fgain, final_norm):
    n_tok = x2d.shape[0]
    return pl.pallas_call(
        functools.partial(_ffn_kernel, final_norm=final_norm),
        grid=(n_tok // FFN_TOKENS,),
        in_specs=[
            pl.BlockSpec((FFN_TOKENS, D_MODEL), lambda i: (i, 0)),
            _resident((1, D_MODEL)),
            _resident((D_MODEL, 2 * D_FF), layer),
            _resident((D_FF, D_MODEL), layer),
            _resident((1, D_MODEL)),
        ],
        out_specs=pl.BlockSpec((FFN_TOKENS, D_MODEL), lambda i: (i, 0)),
        out_shape=jax.ShapeDtypeStruct(x2d.shape, F32),
        compiler_params=pltpu.CompilerParams(
            dimension_semantics=("arbitrary",), vmem_limit_bytes=V7X_VMEM_LIMIT),
        name="ffn_final" if final_norm else "ffn",
    )(x2d, gain, w_in, w_out, fgain)


def _mixer_kernel(sink_ref, x_ref, gain_ref, w_in_ref, bgate_ref, vnorm_ref, ws_ref, bs_ref,
                  gn_ref, wa_ref, wb_ref, wc_ref, wo_ref, decay_ref, kw_ref, qw_ref, bias_ref,
                  o_ref,
                  wsp_s, u_s, v_s, qst_s, k_s, vv_s, cqb_s, cqw_s, ckb_s, ckwt_s, cv_s, sg_s,
                  ya_s, yb_s, yc_s, state_s, stateb_s,
                  *, chunk_decay):
    b = pl.program_id(0)
    t = pl.program_id(1)

    @pl.when(jnp.logical_and(b == 0, t == 0))
    def _():
        row = lax.broadcasted_iota(jnp.int32, (CHUNK, CHUNK), 0)
        col = lax.broadcasted_iota(jnp.int32, (CHUNK, CHUNK), 1)
        for g in range(A_GROUPS):
            wsp_s[g] = jnp.where(row >= col, ws_ref[g], 0.0).astype(BF16)

    @pl.when(t == 0)
    def _():
        state_s[...] = jnp.zeros_like(state_s)
        stateb_s[...] = jnp.zeros_like(stateb_s)
        k_s[:, 0:CHUNK, :] = jnp.zeros((B_KV_HEADS, CHUNK, B_HEAD_DIM), BF16)
        vv_s[:, 0:CHUNK, :] = jnp.zeros((B_KV_HEADS, CHUNK, B_HEAD_DIM), BF16)

    def sub_tile(s):
        r0 = s * SUB_TOKENS
        c0 = s * SUB_CHUNKS
        sub = slice(r0, r0 + SUB_TOKENS)
        x = x_ref[sub, :]
        h = _rmsnorm(x, gain_ref[...]).astype(BF16)

        def proj(lo, width):
            return _dot(h, w_in_ref[:, lo:lo + width])

        def chunk_rows(c):
            return slice((c0 + c) * CHUNK, (c0 + c + 1) * CHUNK)

        def window_rows(c):
            return slice((c0 + c) * CHUNK, (c0 + c + 2) * CHUNK)

        prev_cols = lax.broadcasted_iota(jnp.int32, (1, 2 * CHUNK), 1) < CHUNK
        units_b = [(c, hk) for c in range(SUB_CHUNKS) for hk in range(B_KV_HEADS)]
        units_c = [(c, hd) for c in range(SUB_CHUNKS) for hd in range(C_HEADS)]


        qb = (proj(OFF_BQ, B_WIDTH) * (1.0 / math.sqrt(B_HEAD_DIM))).astype(BF16)
        kb = proj(OFF_BK, B_KV_WIDTH).astype(BF16)
        vb = proj(OFF_BV, B_KV_WIDTH).astype(BF16)
        for hk in range(B_KV_HEADS):
            kcols = slice(hk * B_HEAD_DIM, (hk + 1) * B_HEAD_DIM)
            k_s[hk, CHUNK + r0:CHUNK + r0 + SUB_TOKENS, :] = kb[:, kcols]
            vv_s[hk, CHUNK + r0:CHUNK + r0 + SUB_TOKENS, :] = vb[:, kcols]
            for c in range(SUB_CHUNKS):
                qst_s[(c0 + c) * B_KV_HEADS + hk] = jnp.concatenate(
                    [qb[c * CHUNK:(c + 1) * CHUNK, j * B_HEAD_DIM:(j + 1) * B_HEAD_DIM]
                     for j in range(hk * B_Q_PER_KV, (hk + 1) * B_Q_PER_KV)], axis=0)

        def attention_probs(c, hk):
            k2 = k_s[hk, window_rows(c), :]
            scores = _dot_nt(qst_s[(c0 + c) * B_KV_HEADS + hk], k2) + bias_ref[hk]
            if c0 + c == 0:
                scores = jnp.where(jnp.logical_and(t == 0, prev_cols), -jnp.inf, scores)
            probs, denoms = [], []
            for g in range(B_Q_PER_KV):
                sc = scores[g * CHUNK:(g + 1) * CHUNK]
                sink = sink_ref[hk * B_Q_PER_KV + g]
                m = jnp.maximum(jnp.max(sc, axis=-1, keepdims=True), sink)
                p = jnp.exp(sc - m)
                denoms.append(jnp.sum(p, axis=-1, keepdims=True) + jnp.exp(sink - m))
                probs.append(p.astype(BF16))
            return jnp.concatenate(probs, axis=0), denoms

        probs_b = {u: attention_probs(*u) for u in units_b}

        z = jax.nn.gelu(proj(OFF_AU, 2 * A_WIDTH))
        u_s[sub, :] = z[:, :A_WIDTH]
        v_s[sub, :] = _rmsnorm(z[:, A_WIDTH:], vnorm_ref[...]).astype(BF16)
        cq = proj(OFF_CQ, C_QK_WIDTH)
        cqb_s[sub, :] = cq.astype(BF16)
        cqw_s[sub, :] = (cq * qw_ref[...]).astype(BF16)
        ck = proj(OFF_CK, C_QK_WIDTH) * (C_QK_DIM ** -0.5)
        ckb_s[sub, :] = ck.astype(BF16)
        ckwt_s[:, sub] = (ck * kw_ref[...]).T.astype(BF16)
        cv_s[sub, :] = proj(OFF_CV, C_V_WIDTH).astype(BF16)
        cg = proj(OFF_CG, C_V_WIDTH)
        sg_s[sub, :] = cg * jax.nn.sigmoid(cg)

        for c, hk in units_b:
            probs, denoms = probs_b[(c, hk)]
            out = _dot(probs, vv_s[hk, window_rows(c), :])
            for g in range(B_Q_PER_KV):
                j = hk * B_Q_PER_KV + g
                yb_s[chunk_rows(c), j * B_HEAD_DIM:(j + 1) * B_HEAD_DIM] = (
                    out[g * CHUNK:(g + 1) * CHUNK] / denoms[g]).astype(BF16)

        for c in range(SUB_CHUNKS):
            rows = chunk_rows(c)
            for g in range(A_GROUPS):
                cols = slice(g * CHUNK, (g + 1) * CHUNK)
                mixed = _dot(wsp_s[g], v_s[rows, cols]) + bs_ref[g]
                ya_s[rows, cols] = (u_s[rows, cols] * mixed).astype(BF16)

        def retention_scores(c, hd):
            rows = chunk_rows(c)
            qk = slice(hd * C_QK_DIM, (hd + 1) * C_QK_DIM)
            return (_dot_nt(cqb_s[rows, qk], ckb_s[rows, qk]) * decay_ref[hd]).astype(BF16)

        scores_c = {u: retention_scores(*u) for u in units_c}

        for c, hd in units_c:
            rows = chunk_rows(c)
            qk = slice(hd * C_QK_DIM, (hd + 1) * C_QK_DIM)
            vc = slice(hd * C_V_DIM, (hd + 1) * C_V_DIM)
            vh = cv_s[rows, vc]
            lhs = jnp.concatenate([scores_c[(c, hd)], cqw_s[rows, qk]], axis=1)
            rhs = jnp.concatenate([vh, stateb_s[hd]], axis=0)
            o = _dot(lhs, rhs)
            mu = jnp.mean(o, axis=-1, keepdims=True)
            var = jnp.mean(jnp.square(o - mu), axis=-1, keepdims=True)
            o = (o - mu) * lax.rsqrt(var + GN_EPS) * gn_ref[:, vc]
            yc_s[rows, vc] = (o * sg_s[rows, vc]).astype(BF16)
            state = chunk_decay[hd] * state_s[hd] + _dot(ckwt_s[qk, rows], vh)
            state_s[hd] = state
            stateb_s[hd] = state.astype(BF16)

        merged = None
        for j, (y_s, w_ref) in enumerate(((ya_s, wa_ref), (yb_s, wb_ref), (yc_s, wc_ref))):
            lo = j * D_MODEL
            gate = jax.nn.sigmoid(proj(OFF_GATE + lo, D_MODEL) + bgate_ref[:, lo:lo + D_MODEL])
            term = gate * _dot(y_s[sub, :], w_ref[...])
            merged = term if merged is None else merged + term
        o_ref[sub, :] = x + _dot(merged.astype(BF16), wo_ref[...])

    for s in range(MIX_SUBTILES):
        sub_tile(s)

    k_s[:, 0:CHUNK, :] = k_s[:, MIX_TOKENS:, :]
    vv_s[:, 0:CHUNK, :] = vv_s[:, MIX_TOKENS:, :]


def _retention_constants():
    gamma = 1.0 - np.exp2(-5.0 - np.arange(C_HEADS, dtype=np.float64))
    log_gamma = np.log(gamma)
    idx = np.arange(CHUNK, dtype=np.float64)
    dist = idx[:, None] - idx[None, :]
    decay = np.where(dist[None] >= 0, np.exp(log_gamma[:, None, None] * np.maximum(dist, 0.0)[None]), 0.0)
    k_w = np.exp(log_gamma[:, None] * (CHUNK - 1.0 - idx)[None, :])
    q_w = np.exp(log_gamma[:, None] * (idx + 1.0)[None, :])

    def per_token_cols(w):
        cols = np.repeat(w.T, C_QK_DIM, axis=1)
        return np.tile(cols, (SUB_CHUNKS, 1))

    chunk_decay = tuple(float(v) for v in np.exp(log_gamma * CHUNK))
    as32 = lambda a: jnp.asarray(np.ascontiguousarray(a), dtype=F32)
    return as32(decay), as32(per_token_cols(k_w)), as32(per_token_cols(q_w)), chunk_decay


def _alibi_window_bias():
    slopes = np.exp2(-8.0 / B_Q_HEADS * (np.arange(B_Q_HEADS, dtype=np.float64) + 1.0))
    qpos = np.arange(CHUNK)[:, None] + CHUNK
    kpos = np.arange(2 * CHUNK)[None, :]
    dist = (qpos - kpos).astype(np.float64)
    valid = (dist >= 0) & (dist < CHUNK)
    bias = np.where(valid[None], -slopes[:, None, None] * dist[None], -np.inf)
    return jnp.asarray(bias.reshape(B_KV_HEADS, B_Q_PER_KV * CHUNK, 2 * CHUNK), dtype=F32)


def _mixer(x, gain, w_in, b_gate, v_norm, w_s, b_s, sinks, gn, w_a, w_b, w_c, w_o, layer):
    bsz, t_len, _ = x.shape
    decay, k_w, q_w, chunk_decay = _retention_constants()
    bias = _alibi_window_bias()
    bs_cols = jnp.broadcast_to(b_s[:, :, None], (A_GROUPS, CHUNK, CHUNK))
    tok = MIX_TOKENS
    scratch = [
        pltpu.VMEM((A_GROUPS, CHUNK, CHUNK), BF16),
        pltpu.VMEM((tok, A_WIDTH), F32),
        pltpu.VMEM((tok, A_WIDTH), BF16),
        pltpu.VMEM((MIX_CHUNKS * B_KV_HEADS, B_Q_PER_KV * CHUNK, B_HEAD_DIM), BF16),
        pltpu.VMEM((B_KV_HEADS, tok + CHUNK, B_HEAD_DIM), BF16),
        pltpu.VMEM((B_KV_HEADS, tok + CHUNK, B_HEAD_DIM), BF16),
        pltpu.VMEM((tok, C_QK_WIDTH), BF16),
        pltpu.VMEM((tok, C_QK_WIDTH), BF16),
        pltpu.VMEM((tok, C_QK_WIDTH), BF16),
        pltpu.VMEM((C_QK_WIDTH, tok), BF16),
        pltpu.VMEM((tok, C_V_WIDTH), BF16),
        pltpu.VMEM((tok, C_V_WIDTH), F32),
        pltpu.VMEM((tok, A_WIDTH), BF16),
        pltpu.VMEM((tok, B_WIDTH), BF16),
        pltpu.VMEM((tok, C_V_WIDTH), BF16),
        pltpu.VMEM((C_HEADS, C_QK_DIM, C_V_DIM), F32),
        pltpu.VMEM((C_HEADS, C_QK_DIM, C_V_DIM), BF16),
    ]
    return pl.pallas_call(
        functools.partial(_mixer_kernel, chunk_decay=chunk_decay),
        grid=(bsz, t_len // tok),
        in_specs=[
            pl.BlockSpec(memory_space=pltpu.SMEM),
            pl.BlockSpec((None, tok, D_MODEL), lambda b, t: (b, t, 0)),
            _resident((1, D_MODEL)),
            _resident((D_MODEL, IN_WIDTH), layer),
            _resident((1, 3 * D_MODEL)),
            _resident((1, A_WIDTH)),
            _resident((A_GROUPS, CHUNK, CHUNK)),
            _resident((A_GROUPS, CHUNK, CHUNK)),
            _resident((1, C_V_WIDTH)),
            _resident((A_WIDTH, D_MODEL), layer),
            _resident((B_WIDTH, D_MODEL), layer),
            _resident((C_V_WIDTH, D_MODEL), layer),
            _resident((D_MODEL, D_MODEL), layer),
            _resident((C_HEADS, CHUNK, CHUNK)),
            _resident((SUB_TOKENS, C_QK_WIDTH)),
            _resident((SUB_TOKENS, C_QK_WIDTH)),
            _resident((B_KV_HEADS, B_Q_PER_KV * CHUNK, 2 * CHUNK)),
        ],
        out_specs=pl.BlockSpec((None, tok, D_MODEL), lambda b, t: (b, t, 0)),
        out_shape=jax.ShapeDtypeStruct(x.shape, F32),
        scratch_shapes=scratch,
        compiler_params=pltpu.CompilerParams(
            dimension_semantics=("arbitrary", "arbitrary"), vmem_limit_bytes=V7X_VMEM_LIMIT),
        name="mixer",
    )(sinks, x, gain, w_in, b_gate, v_norm, w_s, bs_cols, gn, w_a, w_b, w_c, w_o,
      decay, k_w, q_w, bias)


def kernel(x, ffn1_norm, ffn1_w_in, ffn1_w_out, mix_norm, w_in, b_gate, gmlp_v_norm, gmlp_w_s,
           gmlp_b_s, attn_sinks, ret_gn, w_branch_a, w_branch_b, w_branch_c, w_out,
           ffn2_norm, ffn2_w_in, ffn2_w_out, final_norm):
    bsz, t_len, d = x.shape
    depth = ffn1_norm.shape[0]
    row = lambda v: v.reshape(1, -1)
    fgain = row(final_norm)
    (f1_in, f1_out, f2_in, f2_out, m_in, m_a, m_b, m_c, m_out) = _weights_to_bf16(
        [ffn1_w_in, ffn1_w_out, ffn2_w_in, ffn2_w_out, w_in, w_branch_a, w_branch_b, w_branch_c, w_out])
    for l in range(depth):
        x2d = _ffn(x.reshape(bsz * t_len, d), row(ffn1_norm[l]), f1_in, f1_out, l, fgain, False)
        x = _mixer(x2d.reshape(bsz, t_len, d), row(mix_norm[l]), m_in,
                   row(b_gate[l]), row(gmlp_v_norm[l]), gmlp_w_s[l], gmlp_b_s[l], attn_sinks[l],
                   row(ret_gn[l]), m_a, m_b, m_c, m_out, l)
        x2d = _ffn(x.reshape(bsz * t_len, d), row(ffn2_norm[l]), f2_in, f2_out, l, fgain,
                   l == depth - 1)
        x = x2d.reshape(bsz, t_len, d)
    return x
```
